```python
import math
import jax, jax.numpy as jnp
from jax import lax
import numpy as np

D_MODEL = 2048
BATCH = 4
SEQ = 2048
DEPTH = 4

MIX = D_MODEL // 4
N_BRANCH = 4
RMS_EPS = 1e-6
ATT_HEADS = 8
ATT_HEAD_DIM = MIX // ATT_HEADS
ROT_DIM = ATT_HEAD_DIM // 4
ROPE_THETA = 500000.0
IDX_HEADS = 4
IDX_DIM = 64
TOPK_MAX = 256
Q_BLOCK = 128
RWKV_HEADS = 8
RWKV_HEAD_DIM = MIX // RWKV_HEADS
DECAY_LORA = 96
AAA_LORA = 96
GATE_LORA = 256
GN_EPS = 64e-5
POOL_GROUPS = 4
POOL_GROUP_DIM = MIX // POOL_GROUPS
POOL_WINDOWS = (2, 4, 8, 16)
SSM_GROUP_DIM = 16
SSM_GROUPS = MIX // SSM_GROUP_DIM
SSM_STATE = 64
D_FF = 5504
CONV_W = 3
A_COLS = 3 * MIX + IDX_HEADS * IDX_DIM + IDX_DIM + IDX_HEADS
B_COLS = 3 * MIX + DECAY_LORA + AAA_LORA + GATE_LORA
C_COLS = MIX
D_COLS = MIX
G_COLS = N_BRANCH * D_MODEL
IN_COLS = A_COLS + B_COLS + C_COLS + D_COLS + G_COLS

kernel_name = "hybrid_gated_dsa_rwkv7_pool_s5"


def _split(z, sizes):
    return jnp.split(z, [int(i) for i in np.cumsum(sizes)[:-1]], axis=-1)


def _shift_right(z, n):
    return jnp.pad(z, ((0, 0), (n, 0), (0, 0)))[:, : z.shape[1]]


def rms_norm(x, g):
    xf = x.astype(jnp.float32)
    y = xf * lax.rsqrt(jnp.mean(xf * xf, axis=-1, keepdims=True) + RMS_EPS)
    return (y * g.astype(jnp.float32)).astype(x.dtype)


def rope_tables(positions, dtype):
    inv = ROPE_THETA ** (-jnp.arange(0, ROT_DIM, 2, dtype=jnp.float32) / ROT_DIM)
    ang = positions.astype(jnp.float32)[..., None] * inv
    return jnp.cos(ang)[:, :, None, :].astype(dtype), jnp.sin(ang)[:, :, None, :].astype(dtype)


def partial_rope(x, cos, sin):
    half = ROT_DIM // 2
    x1 = x[..., :half]
    x2 = x[..., half:ROT_DIM]
    return jnp.concatenate([x1 * cos - x2 * sin, x2 * cos + x1 * sin, x[..., ROT_DIM:]], axis=-1)


def dsa_attention(q, k, v, q_idx, k_idx, w_idx):
    bsz, seq = q.shape[0], q.shape[1]
    n_sel = min(TOPK_MAX, seq // 4)
    n_blk = seq // Q_BLOCK
    key_pos = jnp.arange(seq)
    gather = jax.vmap(lambda src, ix: src[ix])

    def block(i):
        start = i * Q_BLOCK
        qb = lax.dynamic_slice_in_dim(q, start, Q_BLOCK, axis=1)
        qib = lax.dynamic_slice_in_dim(q_idx, start, Q_BLOCK, axis=1)
        wb = lax.dynamic_slice_in_dim(w_idx, start, Q_BLOCK, axis=1).astype(jnp.float32) * IDX_HEADS ** -0.5
        qpos = start + jnp.arange(Q_BLOCK)
        rel = jax.nn.relu(jnp.einsum('bthd,bsd->bths', qib, k_idx).astype(jnp.float32) * IDX_DIM ** -0.5)
        score = jnp.einsum('bth,bths->bts', wb, rel)
        causal = key_pos[None, :] <= qpos[:, None]
        score = jnp.where(causal[None], score, -jnp.inf)
        _, sel = lax.top_k(score, n_sel)
        kg = gather(k, sel)
        vg = gather(v, sel)
        logits = jnp.einsum('bthd,btkhd->bhtk', qb, kg).astype(jnp.float32) * ATT_HEAD_DIM ** -0.5
        valid = sel <= qpos[None, :, None]
        logits = jnp.where(valid[:, None], logits, -jnp.inf)
        p = jax.nn.softmax(logits, axis=-1)
        return jnp.einsum('bhtk,btkhd->bthd', p, vg.astype(jnp.float32))

    out = lax.map(block, jnp.arange(n_blk))
    out = jnp.moveaxis(out, 0, 1).reshape(bsz, seq, ATT_HEADS * ATT_HEAD_DIM)
    return out.astype(q.dtype)


def rwkv7_mix(z, mu, w0, w2, a0, a2, g2, k_k, k_a, r_k, lnx_g, lnx_b):
    bsz, seq = z.shape[0], z.shape[1]
    H, N = RWKV_HEADS, RWKV_HEAD_DIM
    f32 = jnp.float32
    z = z + (_shift_right(z, 1) - z) * mu
    r, k, v, lw, la, lg = _split(z, [MIX, MIX, MIX, DECAY_LORA, AAA_LORA, GATE_LORA])
    wlog = -jax.nn.softplus(-(w0 + jnp.tanh(lw) @ w2)) - 0.5
    decay = jnp.exp(-jnp.exp(wlog.astype(f32)))
    a = jax.nn.sigmoid(a0 + la @ a2)
    g = jax.nn.sigmoid(lg) @ g2
    heads = lambda t: t.astype(f32).reshape(bsz, seq, H, N)
    kk = heads(k * k_k)
    kk = kk / jnp.maximum(jnp.sqrt(jnp.sum(kk * kk, axis=-1, keepdims=True)), 1e-12)
    k = k * (1.0 + (a - 1.0) * k_a)
    rh, kh, vh, wh, ah = heads(r), heads(k), heads(v), heads(decay), heads(a)
    to_t = lambda t: jnp.moveaxis(t, 1, 0)

    def step(state, inp):
        r_t, w_t, k_t, v_t, a_t, b_t = inp
        sa = jnp.einsum('bhij,bhj->bhi', state, a_t)
        state = state * w_t[:, :, None, :] + sa[..., None] * b_t[:, :, None, :] + v_t[..., None] * k_t[:, :, None, :]
        return state, jnp.einsum('bhij,bhj->bhi', state, r_t)

    state0 = jnp.zeros((bsz, H, N, N), f32)
    _, y = lax.scan(step, state0, (to_t(rh), to_t(wh), to_t(kh), to_t(vh), to_t(-kk), to_t(kk * ah)))
    y = jnp.moveaxis(y, 0, 1)
    mean = jnp.mean(y, axis=-1, keepdims=True)
    var = jnp.mean(jnp.square(y - mean), axis=-1, keepdims=True)
    yn = ((y - mean) * lax.rsqrt(var + GN_EPS)).reshape(bsz, seq, MIX) * lnx_g + lnx_b
    bonus = (jnp.sum(rh * kh * r_k.astype(f32).reshape(H, N), axis=-1, keepdims=True) * vh).reshape(bsz, seq, MIX)
    return ((yn + bonus) * g).astype(z.dtype)


def multiscale_pool(z, pool_w, pool_scale):
    f32 = jnp.float32
    seq = z.shape[1]
    zf = z.astype(f32)
    cs = jnp.cumsum(zf, axis=1)
    count_base = jnp.arange(1, seq + 1, dtype=f32)[None, :, None]
    zg = jnp.split(zf, POOL_GROUPS, axis=-1)
    csg = jnp.split(cs, POOL_GROUPS, axis=-1)
    outs = []
    for gi, win in enumerate(POOL_WINDOWS):
        win_sum = csg[gi] - _shift_right(csg[gi], win)
        outs.append(win_sum / jnp.minimum(count_base, float(win)) - zg[gi])
    d = jnp.stack(outs, axis=2)
    y = jnp.einsum('bsgc,gcd->bsgd', d, pool_w.astype(f32)).reshape(z.shape)
    return (y * pool_scale).astype(z.dtype)


def s5_ssm(u, a_re, a_im, log_dt, b_re, b_im, c_re, c_im, d_skip, w_glu, b_glu):
    bsz, seq = u.shape[0], u.shape[1]
    f32 = jnp.float32
    uf = u.astype(f32).reshape(bsz, seq, SSM_GROUPS, SSM_GROUP_DIM)
    lam = lax.complex(a_re.astype(f32), a_im.astype(f32))
    dt = jnp.exp(log_dt.astype(f32))[:, None]
    lam_bar = jnp.exp(lam * dt)
    bmat = lax.complex(b_re.astype(f32), b_im.astype(f32))
    b_bar = ((lam_bar - 1.0) / lam)[..., None] * bmat
    bu = jnp.einsum('gpc,bsgc->bsgp', b_bar, uf.astype(jnp.complex64))
    a_seq = jnp.broadcast_to(lam_bar, bu.shape)

    def combine(e1, e2):
        a1, b1 = e1
        a2, b2 = e2
        return a2 * a1, a2 * b1 + b2

    _, states = lax.associative_scan(combine, (a_seq, bu), axis=1)
    cmat = lax.complex(c_re.astype(f32), c_im.astype(f32))
    y = jnp.real(jnp.einsum('gcp,bsgp->bsgc', cmat, states)) + d_skip.astype(f32).reshape(SSM_GROUPS, SSM_GROUP_DIM) * uf
    y = jax.nn.gelu(y.reshape(bsz, seq, MIX))
    out = y * jax.nn.sigmoid(y @ w_glu.astype(f32) + b_glu)
    return out.astype(u.dtype)


def conv_ffn(h, w_up, conv_w, conv_b, w_down):
    seq = h.shape[1]
    u = h @ w_up
    up = jnp.pad(u, ((0, 0), (CONV_W - 1, 0), (0, 0)))
    uc = conv_b + sum(conv_w[j] * up[:, j:j + seq] for j in range(CONV_W))
    gate, val = jnp.split(uc, 2, axis=-1)
    return (jax.nn.silu(gate) * val) @ w_down


def setup_inputs(seed: int = 0) -> dict:
    key = jax.random.key(seed)
    ks = jax.random.split(key, 40)
    f32 = jnp.float32
    nrm = lambda k, shape, s: jax.random.normal(k, shape, f32) * s
    L = DEPTH
    return {
        "x": nrm(ks[0], (BATCH, SEQ, D_MODEL), 1.0),
        "positions": jnp.arange(SEQ, dtype=jnp.int32)[None, :] + jax.random.randint(ks[1], (BATCH, 1), 0, 1024, dtype=jnp.int32),
        "norm_mix": 1.0 + nrm(ks[2], (L, D_MODEL), 0.02),
        "w_in": nrm(ks[3], (L, D_MODEL, IN_COLS), D_MODEL ** -0.5),
        "b_gate": nrm(ks[4], (L, G_COLS), 0.02),
        "idx_k_norm": 1.0 + nrm(ks[5], (L, IDX_DIM), 0.02),
        "rwkv_mu": jax.random.uniform(ks[6], (L, B_COLS), f32),
        "rwkv_w0": jax.random.uniform(ks[7], (L, MIX), f32, -6.0, 1.0),
        "rwkv_w2": nrm(ks[8], (L, DECAY_LORA, MIX), DECAY_LORA ** -0.5),
        "rwkv_a0": nrm(ks[9], (L, MIX), 0.1),
        "rwkv_a2": nrm(ks[10], (L, AAA_LORA, MIX), AAA_LORA ** -0.5),
        "rwkv_g2": nrm(ks[11], (L, GATE_LORA, MIX), GATE_LORA ** -0.5),
        "rwkv_k_k": 0.85 + nrm(ks[12], (L, MIX), 0.1),
        "rwkv_k_a": 1.0 + nrm(ks[13], (L, MIX), 0.1),
        "rwkv_r_k": nrm(ks[14], (L, MIX), 0.1),
        "rwkv_lnx_g": 1.0 + nrm(ks[15], (L, MIX), 0.02),
        "rwkv_lnx_b": nrm(ks[16], (L, MIX), 0.02),
        "pool_w": nrm(ks[17], (L, POOL_GROUPS, POOL_GROUP_DIM, POOL_GROUP_DIM), POOL_GROUP_DIM ** -0.5),
        "pool_scale": 1.0 + nrm(ks[18], (L, MIX), 0.1),
        "ssm_a_re": -0.5 + nrm(ks[19], (L, SSM_GROUPS, SSM_STATE), 1e-3),
        "ssm_a_im": math.pi * jnp.arange(SSM_STATE, dtype=f32)[None, None, :] + nrm(ks[20], (L, SSM_GROUPS, SSM_STATE), 1e-3),
        "ssm_log_dt": jax.random.uniform(ks[21], (L, SSM_GROUPS), f32, math.log(1e-3), math.log(1e-1)),
        "ssm_b_re": nrm(ks[22], (L, SSM_GROUPS, SSM_STATE, SSM_GROUP_DIM), (2 * SSM_GROUP_DIM) ** -0.5),
        "ssm_b_im": nrm(ks[23], (L, SSM_GROUPS, SSM_STATE, SSM_GROUP_DIM), (2 * SSM_GROUP_DIM) ** -0.5),
        "ssm_c_re": nrm(ks[24], (L, SSM_GROUPS, SSM_GROUP_DIM, SSM_STATE), SSM_STATE ** -0.5),
        "ssm_c_im": nrm(ks[25], (L, SSM_GROUPS, SSM_GROUP_DIM, SSM_STATE), SSM_STATE ** -0.5),
        "ssm_d": nrm(ks[26], (L, MIX), 1.0),
        "ssm_w_glu": nrm(ks[27], (L, MIX, MIX), MIX ** -0.5),
        "ssm_b_glu": nrm(ks[28], (L, MIX), 0.02),
        "w_branch": nrm(ks[29], (L, N_BRANCH, MIX, D_MODEL), MIX ** -0.5),
        "w_out": nrm(ks[30], (L, D_MODEL, D_MODEL), 0.5 * D_MODEL ** -0.5),
        "norm_ffn": 1.0 + nrm(ks[31], (L, D_MODEL), 0.02),
        "w_up": nrm(ks[32], (L, D_MODEL, 2 * D_FF), D_MODEL ** -0.5),
        "conv_w": nrm(ks[33], (L, CONV_W, 2 * D_FF), CONV_W ** -0.5),
        "conv_b": nrm(ks[34], (L, 2 * D_FF), 0.02),
        "w_down": nrm(ks[35], (L, D_FF, D_MODEL), 0.5 * D_FF ** -0.5),
        "norm_final": 1.0 + nrm(ks[36], (D_MODEL,), 0.02),
    }


def reference(x, positions, norm_mix, w_in, b_gate, idx_k_norm, rwkv_mu, rwkv_w0, rwkv_w2, rwkv_a0, rwkv_a2,
              rwkv_g2, rwkv_k_k, rwkv_k_a, rwkv_r_k, rwkv_lnx_g, rwkv_lnx_b, pool_w, pool_scale, ssm_a_re,
              ssm_a_im, ssm_log_dt, ssm_b_re, ssm_b_im, ssm_c_re, ssm_c_im, ssm_d, ssm_w_glu, ssm_b_glu,
              w_branch, w_out, norm_ffn, w_up, conv_w, conv_b, w_down, norm_final):
    bsz, seq = x.shape[0], x.shape[1]
    cos, sin = rope_tables(positions, x.dtype)
    y = x
    for l in range(DEPTH):
        h = rms_norm(y, norm_mix[l])
        z = h @ w_in[l]
        za, zb, zc, zd, zg = _split(z, [A_COLS, B_COLS, C_COLS, D_COLS, G_COLS])
        q, k, v, qi, ki, wi = _split(za, [MIX, MIX, MIX, IDX_HEADS * IDX_DIM, IDX_DIM, IDX_HEADS])
        q = partial_rope(q.reshape(bsz, seq, ATT_HEADS, ATT_HEAD_DIM), cos, sin)
        k = partial_rope(k.reshape(bsz, seq, ATT_HEADS, ATT_HEAD_DIM), cos, sin)
        v = v.reshape(bsz, seq, ATT_HEADS, ATT_HEAD_DIM)
        qi = partial_rope(qi.reshape(bsz, seq, IDX_HEADS, IDX_DIM), cos, sin)
        ki = partial_rope(rms_norm(ki, idx_k_norm[l])[:, :, None, :], cos, sin)[:, :, 0, :]
        ya = dsa_attention(q, k, v, qi, ki, wi)
        yb = rwkv7_mix(zb, rwkv_mu[l], rwkv_w0[l], rwkv_w2[l], rwkv_a0[l], rwkv_a2[l], rwkv_g2[l],
                       rwkv_k_k[l], rwkv_k_a[l], rwkv_r_k[l], rwkv_lnx_g[l], rwkv_lnx_b[l])
        yc = multiscale_pool(zc, pool_w[l], pool_scale[l])
        yd = s5_ssm(zd, ssm_a_re[l], ssm_a_im[l], ssm_log_dt[l], ssm_b_re[l], ssm_b_im[l],
                    ssm_c_re[l], ssm_c_im[l], ssm_d[l], ssm_w_glu[l], ssm_b_glu[l])
        mix = jnp.stack([ya, yb, yc, yd], axis=2)
        proj = jnp.einsum('bsnc,ncd->bsnd', mix, w_branch[l])
        gates = jax.nn.sigmoid(zg + b_gate[l]).reshape(bsz, seq, N_BRANCH, D_MODEL)
        merged = jnp.einsum('bsnd,bsnd->bsd', gates, proj)
        y = y + merged @ w_out[l]
        y = y + conv_ffn(rms_norm(y, norm_ffn[l]), w_up[l], conv_w[l], conv_b[l], w_down[l])
    return rms_norm(y, norm_final)
```

```python
import functools
import math

import jax
import jax.numpy as jnp
import numpy as np
from jax import lax
from jax.experimental import pallas as pl
from jax.experimental.pallas import tpu as pltpu

F32 = jnp.float32
BF16 = jnp.bfloat16
HIGHEST = lax.Precision.HIGHEST

D_MODEL = 2048
MIX = 512
N_BRANCH = 4
RMS_EPS = 1e-6
ATT_HEADS = 8
ATT_HEAD_DIM = 64
ROT_DIM = 16
ROPE_THETA = 500000.0
IDX_HEADS = 4
IDX_DIM = 64
TOPK_MAX = 256
RWKV_HEADS = 8
RWKV_HEAD_DIM = 64
DECAY_LORA = 96
AAA_LORA = 96
GATE_LORA = 256
GN_EPS = 64e-5
POOL_GROUPS = 4
POOL_WINDOWS = (2, 4, 8, 16)
SSM_GROUP_DIM = 16
SSM_GROUPS = 32
SSM_STATE = 64
D_FF = 5504
CONV_W = 3
A_COLS = 3 * MIX + IDX_HEADS * IDX_DIM + IDX_DIM + IDX_HEADS
B_COLS = 3 * MIX + DECAY_LORA + AAA_LORA + GATE_LORA
G_COLS = N_BRANCH * D_MODEL

LANES = 128
SUBLANES = 8
VMEM_LIMIT = 56 * 1024 * 1024
FF_TILE = 256
D_FF_PAD = ((D_FF + FF_TILE - 1) // FF_TILE) * FF_TILE
S5_CHUNK = 16
NEG_BIG = -1e30
INT_MIN = -2 ** 31


def _cparams(*sem):
    return pltpu.CompilerParams(dimension_semantics=sem, vmem_limit_bytes=VMEM_LIMIT)


def _nmm_kernel(x_ref, g_ref, w_ref, b_ref, o_ref, h_ref, *, sigmoid):
    @pl.when(pl.program_id(1) == 0)
    def _():
        x = x_ref[...]
        ms = jnp.mean(x * x, axis=-1, keepdims=True)
        h_ref[...] = (x * lax.rsqrt(ms + RMS_EPS) * g_ref[...]).astype(BF16)

    acc = jnp.dot(h_ref[...], w_ref[...], preferred_element_type=F32) + b_ref[...]
    if sigmoid:
        acc = jax.nn.sigmoid(acc)
    o_ref[...] = acc.astype(o_ref.dtype)


def norm_matmul(x, gain, w, bias=None, sigmoid=False, tm=1024, tn=512):
    m, k = x.shape
    n = w.shape[1]
    assert m % tm == 0 and n % tn == 0
    if bias is None:
        bias = jnp.zeros((n,), F32)
    return pl.pallas_call(
        functools.partial(_nmm_kernel, sigmoid=sigmoid),
        grid=(m // tm, n // tn),
        in_specs=[
            pl.BlockSpec((tm, k), lambda i, j: (i, 0)),
            pl.BlockSpec((1, k), lambda i, j: (0, 0)),
            pl.BlockSpec((k, tn), lambda i, j: (0, j)),
            pl.BlockSpec((1, tn), lambda i, j: (0, j)),
        ],
        out_specs=pl.BlockSpec((tm, tn), lambda i, j: (i, j)),
        out_shape=jax.ShapeDtypeStruct((m, n), F32),
        scratch_shapes=[pltpu.VMEM((tm, k), BF16)],
        compiler_params=_cparams("parallel", "arbitrary"),
        name="norm_matmul",
    )(x, gain.reshape(1, k), w, bias.reshape(1, n))


def _mm_res_kernel(a_ref, w_ref, r_ref, o_ref):
    o_ref[...] = r_ref[...] + jnp.dot(a_ref[...], w_ref[...], preferred_element_type=F32)


def matmul_residual(a, w, res, tm=1024, tn=512):
    m, k = a.shape
    n = w.shape[1]
    return pl.pallas_call(
        _mm_res_kernel,
        grid=(m // tm, n // tn),
        in_specs=[
            pl.BlockSpec((tm, k), lambda i, j: (i, 0)),
            pl.BlockSpec((k, tn), lambda i, j: (0, j)),
            pl.BlockSpec((tm, tn), lambda i, j: (i, j)),
        ],
        out_specs=pl.BlockSpec((tm, tn), lambda i, j: (i, j)),
        out_shape=jax.ShapeDtypeStruct((m, n), F32),
        compiler_params=_cparams("parallel", "arbitrary"),
        name="matmul_residual",
    )(a, w, res)


def _rms_kernel(x_ref, g_ref, o_ref):
    x = x_ref[...]
    ms = jnp.mean(x * x, axis=-1, keepdims=True)
    o_ref[...] = x * lax.rsqrt(ms + RMS_EPS) * g_ref[...]


def rms_norm_final(x, gain, tm=512):
    m, k = x.shape
    return pl.pallas_call(
        _rms_kernel,
        grid=(m // tm,),
        in_specs=[pl.BlockSpec((tm, k), lambda i: (i, 0)), pl.BlockSpec((1, k), lambda i: (0, 0))],
        out_specs=pl.BlockSpec((tm, k), lambda i: (i, 0)),
        out_shape=jax.ShapeDtypeStruct((m, k), F32),
        compiler_params=_cparams("parallel"),
        name="rms_norm_final",
    )(x, gain.reshape(1, k))


def _rope_tab_kernel(pos_ref, inv_ref, ma_ref, mb_ref, c_ref, sa_ref, sb_ref):
    ang = pos_ref[...] * inv_ref[...]
    c_ref[...] = jnp.cos(ang)
    s = jnp.sin(ang)
    sa_ref[...] = s * ma_ref[...]
    sb_ref[...] = s * mb_ref[...]


def rope_tables(pos_f32, tm=1024):
    m = pos_f32.shape[0]
    half = ROT_DIM // 2
    f = np.arange(LANES) % ATT_HEAD_DIM
    inv_base = ROPE_THETA ** (-jnp.arange(0, ROT_DIM, 2, dtype=F32) / ROT_DIM)
    inv = jnp.where(jnp.asarray(f < ROT_DIM), inv_base[f % half], 0.0)
    ma = np.where(f < half, -1.0, 0.0).astype(np.float32)
    mb = np.where((f >= half) & (f < ROT_DIM), 1.0, 0.0).astype(np.float32)
    row = pl.BlockSpec((1, LANES), lambda i: (0, 0))
    blk = pl.BlockSpec((tm, LANES), lambda i: (i, 0))
    return pl.pallas_call(
        _rope_tab_kernel,
        grid=(m // tm,),
        in_specs=[pl.BlockSpec((tm, 1), lambda i: (i, 0)), row, row, row],
        out_specs=[blk, blk, blk],
        out_shape=[jax.ShapeDtypeStruct((m, LANES), F32)] * 3,
        compiler_params=_cparams("parallel"),
        name="rope_tables",
    )(pos_f32, inv.reshape(1, LANES), jnp.asarray(ma).reshape(1, LANES),
      jnp.asarray(mb).reshape(1, LANES))


def _rope_apply(x, c, sa, sb):
    n = x.shape[1] // LANES
    half = ROT_DIM // 2
    if n > 1:
        c = jnp.concatenate([c] * n, axis=1)
        sa = jnp.concatenate([sa] * n, axis=1)
        sb = jnp.concatenate([sb] * n, axis=1)
    w = x.shape[1]
    up = pltpu.roll(x, w - half, axis=1)
    dn = pltpu.roll(x, half, axis=1)
    return x * c + up * sa + dn * sb


def _dsa_prep_kernel(za_ref, c_ref, sa_ref, sb_ref, kn_ref, q_ref, k_ref, v_ref, qi_ref, ki_ref, wi_ref):
    c, sa, sb = c_ref[...], sa_ref[...], sb_ref[...]
    q = _rope_apply(za_ref[:, 0:MIX], c, sa, sb)
    q_ref[...] = (q * ATT_HEAD_DIM ** -0.5).astype(BF16)
    k_ref[...] = _rope_apply(za_ref[:, MIX:2 * MIX], c, sa, sb).astype(BF16)
    v_ref[...] = za_ref[:, 2 * MIX:3 * MIX].astype(BF16)
    qi = _rope_apply(za_ref[:, 3 * MIX:3 * MIX + 256], c, sa, sb)
    qi_ref[...] = (qi * IDX_DIM ** -0.5).astype(BF16)
    ki = za_ref[:, 3 * MIX + 256:3 * MIX + 384]
    ms = jnp.sum(ki * ki, axis=-1, keepdims=True) * (1.0 / IDX_DIM)
    ki = ki * lax.rsqrt(ms + RMS_EPS) * kn_ref[...]
    ki = _rope_apply(ki, c, sa, sb)
    ki = ki + pltpu.roll(ki, IDX_DIM, axis=1)
    ki_ref[...] = ki.astype(BF16)
    wi_ref[...] = za_ref[:, 3 * MIX + 384:3 * MIX + 512] * IDX_HEADS ** -0.5


def dsa_prep(za, c, sa, sb, k_norm, tm=512):
    m = za.shape[0]
    kn = jnp.concatenate([k_norm, jnp.zeros((LANES - IDX_DIM,), F32)]).reshape(1, LANES)
    tab = pl.BlockSpec((tm, LANES), lambda i: (i, 0))
    mk = lambda n: pl.BlockSpec((tm, n), lambda i: (i, 0))
    return pl.pallas_call(
        _dsa_prep_kernel,
        grid=(m // tm,),
        in_specs=[pl.BlockSpec((tm, 2048), lambda i: (i, 0)), tab, tab, tab,
                  pl.BlockSpec((1, LANES), lambda i: (0, 0))],
        out_specs=[mk(MIX), mk(MIX), mk(MIX), mk(256), mk(LANES), mk(LANES)],
        out_shape=[jax.ShapeDtypeStruct((m, MIX), BF16)] * 3
        + [jax.ShapeDtypeStruct((m, 256), BF16), jax.ShapeDtypeStruct((m, LANES), BF16),
           jax.ShapeDtypeStruct((m, LANES), F32)],
        compiler_params=_cparams("parallel"),
        name="dsa_prep",
    )(za, c, sa, sb, kn)


def _dsa_attn_kernel(q_ref, qi_ref, wi_ref, k_ref, v_ref, ki_ref, tri_ref, o_ref, key_ref, bias_ref, *, tq, seq, n_sel):
    qt = pl.program_id(1)
    lane = lax.broadcasted_iota(jnp.int32, (tq, LANES), 1)
    lo_half = lane < IDX_DIM
    nt = (((1,), (1,)), ((), ()))

    ki = ki_ref[...]
    wi = wi_ref[...]
    score = jnp.zeros((tq, seq), F32)
    zero_b = jnp.zeros((tq, LANES), BF16)
    for h in range(IDX_HEADS):
        chunk = qi_ref[:, (h // 2) * LANES:(h // 2 + 1) * LANES]
        qh = jnp.where(lo_half if h % 2 == 0 else jnp.logical_not(lo_half), chunk, zero_b)
        s = lax.dot_general(qh, ki, nt, preferred_element_type=F32)
        score = score + wi[:, h:h + 1] * jnp.maximum(s, 0.0)
    qpos = qt * tq + lax.broadcasted_iota(jnp.int32, (tq, seq), 0)
    kpos = lax.broadcasted_iota(jnp.int32, (tq, seq), 1)
    causal = kpos <= qpos
    score = jnp.where(score == 0.0, 0.0, score)
    score = jnp.where(causal, score, -jnp.inf)
    bits = pltpu.bitcast(score, jnp.int32)
    key_ref[...] = jnp.where(bits < 0, bits ^ jnp.int32(0x7FFFFFFF), bits)

    def bit_step(it, thr):
        cand = thr + jnp.left_shift(jnp.int32(1), 31 - it)
        cnt = jnp.sum(jnp.where(key_ref[...] >= cand, 1.0, 0.0), axis=-1, keepdims=True)
        return jnp.where(cnt >= n_sel, cand, thr)

    thr = lax.fori_loop(0, 32, bit_step, jnp.full((tq, 1), INT_MIN, jnp.int32))

    need = n_sel - jnp.sum(jnp.where(key_ref[...] > thr, 1.0, 0.0), axis=-1, keepdims=True)
    tri = tri_ref[...]
    ones = jnp.ones((LANES, LANES), BF16)
    run = jnp.zeros((tq, LANES), F32)
    qpos_c = qt * tq + lax.broadcasted_iota(jnp.int32, (tq, LANES), 0)
    for cidx in range(seq // LANES):
        sl = slice(cidx * LANES, (cidx + 1) * LANES)
        kc = key_ref[:, sl]
        eq = kc == thr
        e = jnp.where(eq, 1.0, 0.0).astype(BF16)
        pre = jnp.dot(e, tri, preferred_element_type=F32) + run
        sel = (kc > thr) | (eq & (pre <= need))
        sel = sel & (lane + cidx * LANES <= qpos_c)
        bias_ref[:, sl] = jnp.where(sel, 0.0, NEG_BIG)
        run = run + jnp.dot(e, ones, preferred_element_type=F32)

    for cidx in range(ATT_HEADS // 2):
        sl = slice(cidx * LANES, (cidx + 1) * LANES)
        qc = q_ref[:, sl]
        kc = k_ref[:, sl]
        vc = v_ref[:, sl]
        acc = jnp.zeros((tq, LANES), F32)
        for hh in range(2):
            m_h = lo_half if hh == 0 else jnp.logical_not(lo_half)
            qh = jnp.where(m_h, qc, zero_b)
            logits = lax.dot_general(qh, kc, nt, preferred_element_type=F32) + bias_ref[...]
            mx = jnp.max(logits, axis=-1, keepdims=True)
            p = jnp.exp(logits - mx)
            den = jnp.sum(p, axis=-1, keepdims=True)
            pv = jnp.dot(p.astype(BF16), vc, preferred_element_type=F32)
            acc = acc + jnp.where(m_h, pv / den, 0.0)
        o_ref[:, sl] = acc


def dsa_attention(q, k, v, qi, ki, wi, bsz, seq, tq=128):
    n_sel = min(TOPK_MAX, seq // 4)
    nq = seq // tq
    tri = jnp.asarray(np.triu(np.ones((LANES, LANES), np.float32)), BF16)
    qspec = lambda n: pl.BlockSpec((tq, n), lambda b, i: (b * nq + i, 0))
    kspec = lambda n: pl.BlockSpec((seq, n), lambda b, i: (b, 0))
    return pl.pallas_call(
        functools.partial(_dsa_attn_kernel, tq=tq, seq=seq, n_sel=n_sel),
        grid=(bsz, nq),
        in_specs=[qspec(MIX), qspec(256), qspec(LANES), kspec(MIX), kspec(MIX), kspec(LANES),
                  pl.BlockSpec((LANES, LANES), lambda b, i: (0, 0))],
        out_specs=qspec(MIX),
        out_shape=jax.ShapeDtypeStruct((bsz * seq, MIX), F32),
        scratch_shapes=[pltpu.VMEM((tq, seq), jnp.int32), pltpu.VMEM((tq, seq), F32)],
        compiler_params=_cparams("parallel", "arbitrary"),
        name="dsa_attention",
    )(q, qi, wi, k, v, ki, tri)


def _rwkv_prep_kernel(z_ref, zp_ref, mu_ref, w0_ref, w2_ref, a0_ref, a2_ref, g2_ref, kk_ref, ka_ref, rk_ref,
                      hsum_ref, r_o, w_o, k_o, v_o, a_o, b_o, g_o, bonus_o, *, tm, seq):
    i = pl.program_id(0)
    z = z_ref[...]
    prev_last = zp_ref[SUBLANES - 1:SUBLANES, :]
    at_start = (i * tm) % seq == 0
    prev_last = jnp.where(at_start, 0.0, prev_last)
    row = lax.broadcasted_iota(jnp.int32, z.shape, 0)
    zs = jnp.where(row == 0, prev_last, pltpu.roll(z, 1, axis=0))
    z = z + (zs - z) * mu_ref[...]
    r = z[:, 0:MIX]
    k = z[:, MIX:2 * MIX]
    v = z[:, 2 * MIX:3 * MIX]
    lw = z[:, 3 * MIX:3 * MIX + LANES]
    la = z[:, 3 * MIX + LANES:3 * MIX + 2 * LANES]
    lg = z[:, 3 * MIX + 2 * LANES:3 * MIX + 2 * LANES + GATE_LORA]
    hsum = hsum_ref[...]
    wpre = w0_ref[...] + jnp.dot(jnp.tanh(lw), w2_ref[...], precision=HIGHEST, preferred_element_type=F32)
    wlog = -_softplus(-wpre) - 0.5
    decay = jnp.exp(-jnp.exp(wlog))
    a = jax.nn.sigmoid(a0_ref[...] + jnp.dot(la, a2_ref[...], precision=HIGHEST, preferred_element_type=F32))
    g = jnp.dot(jax.nn.sigmoid(lg), g2_ref[...], precision=HIGHEST, preferred_element_type=F32)
    kk = k * kk_ref[...]
    ss = jnp.dot(kk * kk, hsum, precision=HIGHEST, preferred_element_type=F32)
    kk = kk / jnp.maximum(jnp.sqrt(ss), 1e-12)
    k2 = k * (1.0 + (a - 1.0) * ka_ref[...])
    rk = jnp.dot(r * k2 * rk_ref[...], hsum, precision=HIGHEST, preferred_element_type=F32)
    r_o[...] = r
    w_o[...] = decay
    k_o[...] = k2
    v_o[...] = v
    a_o[...] = -kk
    b_o[...] = kk * a
    g_o[...] = g
    bonus_o[...] = rk * v


def _softplus(x):
    return jnp.maximum(x, 0.0) + jnp.log(1.0 + jnp.exp(-jnp.abs(x)))


def _head_sum_matrix():
    idx = np.arange(MIX) // RWKV_HEAD_DIM
    return jnp.asarray((idx[:, None] == idx[None, :]).astype(np.float32))


def rwkv_prep(zb, p, seq, tm=256):
    m = zb.shape[0]
    row = lambda n: pl.BlockSpec((1, n), lambda i: (0, 0))
    full = lambda a, b: pl.BlockSpec((a, b), lambda i: (0, 0))
    blk = pl.BlockSpec((tm, MIX), lambda i: (i, 0))
    nprev = tm // SUBLANES
    return pl.pallas_call(
        functools.partial(_rwkv_prep_kernel, tm=tm, seq=seq),
        grid=(m // tm,),
        in_specs=[pl.BlockSpec((tm, 2048), lambda i: (i, 0)),
                  pl.BlockSpec((SUBLANES, 2048), lambda i: (jnp.maximum(i * nprev - 1, 0), 0)),
                  row(2048), row(MIX), full(LANES, MIX), row(MIX), full(LANES, MIX), full(GATE_LORA, MIX),
                  row(MIX), row(MIX), row(MIX), full(MIX, MIX)],
        out_specs=[blk] * 8,
        out_shape=[jax.ShapeDtypeStruct((m, MIX), F32)] * 8,
        compiler_params=_cparams("parallel"),
        name="rwkv_prep",
    )(zb, zb, p["mu"], p["w0"], p["w2"], p["a0"], p["a2"], p["g2"], p["k_k"], p["k_a"], p["r_k"],
      _head_sum_matrix())


RW_ILO = 16


def _rwkv_scan_kernel(r_ref, w_ref, k_ref, a_ref, b_ref, v_ref, y_ref, s_ref, *, tb):
    @pl.when(pl.program_id(0) == 0)
    def _():
        s_ref[...] = jnp.zeros_like(s_ref)

    def step(t, carry):
        rt = r_ref[t]
        wt = w_ref[t]
        kt = k_ref[t]
        at = a_ref[t]
        bt = b_ref[t]
        vt = v_ref[t]
        for il in range(RW_ILO):
            s = s_ref[il]
            sa = jnp.sum(s * at, axis=0, keepdims=True)
            s = s * wt + sa * bt + vt[il:il + 1, :] * kt
            s_ref[il] = s
            y_ref[t, pl.ds(il, 1), :] = jnp.sum(s * rt, axis=0, keepdims=True)
        return carry

    lax.fori_loop(0, tb, step, 0)


def rwkv_scan(r, w, k, a, b, v, tb=64):
    t = r.shape[0]
    cspec = pl.BlockSpec((tb, RWKV_HEAD_DIM, LANES), lambda i: (i, 0, 0))
    vspec = pl.BlockSpec((tb, RW_ILO, LANES), lambda i: (i, 0, 0))
    return pl.pallas_call(
        functools.partial(_rwkv_scan_kernel, tb=tb),
        grid=(t // tb,),
        in_specs=[cspec] * 5 + [vspec],
        out_specs=vspec,
        out_shape=jax.ShapeDtypeStruct((t, RW_ILO, LANES), F32),
        scratch_shapes=[pltpu.VMEM((RW_ILO, RWKV_HEAD_DIM, LANES), F32)],
        compiler_params=_cparams("arbitrary"),
        name="rwkv_scan",
    )(r, w, k, a, b, v)


def _to_scan_coef(x, bsz, seq):
    x = x.reshape(bsz, seq, RWKV_HEADS, RWKV_HEAD_DIM)
    x = jnp.transpose(x, (1, 3, 0, 2)).reshape(seq, RWKV_HEAD_DIM, bsz * RWKV_HEADS)
    return jnp.tile(x, (1, 1, LANES // (bsz * RWKV_HEADS)))


def _to_scan_val(x, bsz, seq):
    nhi = RWKV_HEAD_DIM // RW_ILO
    x = x.reshape(bsz, seq, RWKV_HEADS, nhi, RW_ILO)
    return jnp.transpose(x, (1, 4, 3, 0, 2)).reshape(seq, RW_ILO, LANES)


def _from_scan_val(y, bsz, seq):
    nhi = RWKV_HEAD_DIM // RW_ILO
    y = y.reshape(seq, RW_ILO, nhi, bsz, RWKV_HEADS)
    return jnp.transpose(y, (3, 0, 4, 2, 1)).reshape(bsz * seq, MIX)


def _rwkv_post_kernel(y_ref, bonus_ref, g_ref, lg_ref, lb_ref, hsum_ref, o_ref):
    y = y_ref[...]
    hsum = hsum_ref[...]
    inv = 1.0 / RWKV_HEAD_DIM
    mean = jnp.dot(y, hsum, precision=HIGHEST, preferred_element_type=F32) * inv
    d = y - mean
    var = jnp.dot(d * d, hsum, precision=HIGHEST, preferred_element_type=F32) * inv
    yn = d * lax.rsqrt(var + GN_EPS) * lg_ref[...] + lb_ref[...]
    o_ref[...] = (yn + bonus_ref[...]) * g_ref[...]


def rwkv_post(y, bonus, g, lnx_g, lnx_b, tm=512):
    m = y.shape[0]
    blk = pl.BlockSpec((tm, MIX), lambda i: (i, 0))
    row = pl.BlockSpec((1, MIX), lambda i: (0, 0))
    return pl.pallas_call(
        _rwkv_post_kernel,
        grid=(m // tm,),
        in_specs=[blk, blk, blk, row, row, pl.BlockSpec((MIX, MIX), lambda i: (0, 0))],
        out_specs=blk,
        out_shape=jax.ShapeDtypeStruct((m, MIX), F32),
        compiler_params=_cparams("parallel"),
        name="rwkv_post",
    )(y, bonus, g, lnx_g.reshape(1, MIX), lnx_b.reshape(1, MIX), _head_sum_matrix())


def _pool_kernel(z_ref, w_ref, sc_ref, o_ref, *, seq):
    g = pl.program_id(1)
    z = z_ref[...]
    row = lax.broadcasted_iota(jnp.int32, z.shape, 0)

    def shifted(x, n):
        return jnp.where(row >= n, pltpu.roll(x, n, axis=0), 0.0)

    s2 = z + shifted(z, 1)
    s4 = s2 + shifted(s2, 2)
    s8 = s4 + shifted(s4, 4)
    s16 = s8 + shifted(s8, 8)
    win_sum = jnp.where(g == 0, s2, jnp.where(g == 1, s4, jnp.where(g == 2, s8, s16)))
    win = jnp.left_shift(jnp.int32(2), g)
    cnt = jnp.minimum(row + 1, win).astype(F32)
    d = win_sum / cnt - z
    y = jnp.dot(d.astype(BF16), w_ref[0], preferred_element_type=F32)
    o_ref[...] = y * sc_ref[...]


def multiscale_pool(zcd, pool_w, pool_scale, bsz, seq):
    gd = MIX // POOL_GROUPS
    return pl.pallas_call(
        functools.partial(_pool_kernel, seq=seq),
        grid=(bsz, POOL_GROUPS),
        in_specs=[pl.BlockSpec((seq, gd), lambda b, g: (b, g)),
                  pl.BlockSpec((1, gd, gd), lambda b, g: (g, 0, 0)),
                  pl.BlockSpec((1, gd), lambda b, g: (0, g))],
        out_specs=pl.BlockSpec((seq, gd), lambda b, g: (b, g)),
        out_shape=jax.ShapeDtypeStruct((bsz * seq, MIX), F32),
        compiler_params=_cparams("parallel", "arbitrary"),
        name="multiscale_pool",
    )(zcd, pool_w.astype(BF16), pool_scale.reshape(1, MIX))


def _s5_matrices(a_re, a_im, log_dt, b_re, b_im, c_re, c_im):
    c = S5_CHUNK
    lam = lax.complex(a_re.astype(F32), a_im.astype(F32))
    dt = jnp.exp(log_dt.astype(F32))[:, None]
    lam_bar = jnp.exp(lam * dt)
    b_bar = ((lam_bar - 1.0) / lam)[..., None] * lax.complex(b_re.astype(F32), b_im.astype(F32))
    cmat = lax.complex(c_re.astype(F32), c_im.astype(F32))
    pows = jnp.exp((lam * dt)[:, None, :] * jnp.arange(c + 1, dtype=F32)[None, :, None])
    kd = jnp.real(jnp.einsum("gop,gdp,gpi->gdoi", cmat, pows[:, :c], b_bar))
    tt = np.arange(c)
    dmat = tt[None, :] - tt[:, None]
    kt = kd[:, np.clip(dmat, 0, c - 1)]
    kt = jnp.where((dmat >= 0)[None, :, :, None, None], kt, 0.0)
    m1 = jnp.transpose(kt, (0, 1, 4, 2, 3)).reshape(SSM_GROUPS, c * SSM_GROUP_DIM, c * SSM_GROUP_DIM)
    win = jnp.einsum("grp,gpi->grip", pows[:, c - 1 - tt], b_bar)
    win = win.reshape(SSM_GROUPS, c * SSM_GROUP_DIM, SSM_STATE)
    w_in = jnp.concatenate([jnp.real(win), jnp.imag(win)], axis=-1)
    vout = jnp.einsum("gop,gtp->gpto", cmat, pows[:, 1:c + 1]).reshape(SSM_GROUPS, SSM_STATE, c * SSM_GROUP_DIM)
    v_out = jnp.concatenate([jnp.real(vout), -jnp.imag(vout)], axis=1)
    l1, l2 = [], []
    for lvl in range(7):
        lc = jnp.exp(lam * dt * float(c * 2 ** lvl))
        l1.append(jnp.concatenate([jnp.real(lc), jnp.real(lc)], axis=-1))
        l2.append(jnp.concatenate([-jnp.imag(lc), jnp.imag(lc)], axis=-1))
    return m1, w_in, v_out, jnp.stack(l1, axis=1), jnp.stack(l2, axis=1)


def _s5_core_kernel(u_ref, m1_ref, win_ref, vout_ref, l1_ref, l2_ref, y_ref, *, nchunk):
    u = u_ref[0]
    x = jnp.dot(u, win_ref[0], preferred_element_type=F32)
    row = lax.broadcasted_iota(jnp.int32, x.shape, 0) % nchunk
    l1 = l1_ref[0]
    l2 = l2_ref[0]
    d = 1
    lvl = 0
    while d < nchunk:
        xs = jnp.where(row >= d, pltpu.roll(x, d, axis=0), 0.0)
        x = x + xs * l1[lvl:lvl + 1, :] + pltpu.roll(xs, SSM_STATE, axis=1) * l2[lvl:lvl + 1, :]
        d *= 2
        lvl += 1
    x0 = jnp.where(row >= 1, pltpu.roll(x, 1, axis=0), 0.0)
    y = jnp.dot(u, m1_ref[0], preferred_element_type=F32)
    y = y + jnp.dot(x0.astype(BF16), vout_ref[0], preferred_element_type=F32)
    y_ref[0] = y


def s5_core(uf, m1, w_in, v_out, l1, l2, nchunk):
    g, rows, width = uf.shape
    return pl.pallas_call(
        functools.partial(_s5_core_kernel, nchunk=nchunk),
        grid=(g,),
        in_specs=[pl.BlockSpec((1, rows, width), lambda i: (i, 0, 0)),
                  pl.BlockSpec((1, width, width), lambda i: (i, 0, 0)),
                  pl.BlockSpec((1, width, 2 * SSM_STATE), lambda i: (i, 0, 0)),
                  pl.BlockSpec((1, 2 * SSM_STATE, width), lambda i: (i, 0, 0)),
                  pl.BlockSpec((1, 7, 2 * SSM_STATE), lambda i: (i, 0, 0)),
                  pl.BlockSpec((1, 7, 2 * SSM_STATE), lambda i: (i, 0, 0))],
        out_specs=pl.BlockSpec((1, rows, width), lambda i: (i, 0, 0)),
        out_shape=jax.ShapeDtypeStruct((g, rows, width), F32),
        compiler_params=_cparams("parallel"),
        name="s5_core",
    )(uf, m1, w_in, v_out, l1, l2)


def _s5_post_kernel(y_ref, u_ref, d_ref, wg_ref, bg_ref, o_ref):
    y = y_ref[...] + d_ref[...] * u_ref[...]
    y = jax.nn.gelu(y)
    gate = jnp.dot(y.astype(BF16), wg_ref[...], preferred_element_type=F32) + bg_ref[...]
    o_ref[...] = y * jax.nn.sigmoid(gate)


def s5_post(ycore, zcd, d_skip, w_glu, b_glu, tm=512):
    m = ycore.shape[0]
    blk = pl.BlockSpec((tm, MIX), lambda i: (i, 0))
    row = pl.BlockSpec((1, MIX), lambda i: (0, 0))
    return pl.pallas_call(
        _s5_post_kernel,
        grid=(m // tm,),
        in_specs=[blk, pl.BlockSpec((tm, MIX), lambda i: (i, 1)), row,
                  pl.BlockSpec((MIX, MIX), lambda i: (0, 0)), row],
        out_specs=blk,
        out_shape=jax.ShapeDtypeStruct((m, MIX), F32),
        compiler_params=_cparams("parallel"),
        name="s5_post",
    )(ycore, zcd, d_skip.reshape(1, MIX), w_glu.astype(BF16), b_glu.reshape(1, MIX))


def _merge_kernel(ya_ref, yb_ref, yc_ref, yd_ref, g_ref, wb_ref, o_ref):
    acc = None
    for n, y_ref in enumerate((ya_ref, yb_ref, yc_ref, yd_ref)):
        proj = jnp.dot(y_ref[...].astype(BF16), wb_ref[n], preferred_element_type=F32)
        term = g_ref[:, n * D_MODEL:(n + 1) * D_MODEL] * proj
        acc = term if acc is None else acc + term
    o_ref[...] = acc.astype(BF16)


def gated_merge(ya, yb, yc, yd, gates, w_branch, tm=256):
    m = ya.shape[0]
    blk = pl.BlockSpec((tm, MIX), lambda i: (i, 0))
    return pl.pallas_call(
        _merge_kernel,
        grid=(m // tm,),
        in_specs=[blk, blk, blk, blk, pl.BlockSpec((tm, G_COLS), lambda i: (i, 0)),
                  pl.BlockSpec((N_BRANCH, MIX, D_MODEL), lambda i: (0, 0, 0))],
        out_specs=pl.BlockSpec((tm, D_MODEL), lambda i: (i, 0)),
        out_shape=jax.ShapeDtypeStruct((m, D_MODEL), BF16),
        compiler_params=_cparams("parallel"),
        name="gated_merge",
    )(ya, yb, yc, yd, gates, w_branch)


FFN_HALO = 16


def _ffn_kernel(y_ref, yp_ref, g_ref, wu_ref, cw_ref, cb_ref, wd_ref, o_ref, h_ref, u_ref, *, tm, seq):
    i = pl.program_id(0)
    j = pl.program_id(1)

    def normed(x):
        ms = jnp.mean(x * x, axis=-1, keepdims=True)
        return x * lax.rsqrt(ms + RMS_EPS) * g_ref[...]

    @pl.when(j == 0)
    def _():
        at_start = (i * tm) % seq == 0
        hp = jnp.where(at_start, 0.0, normed(yp_ref[...]))
        h_ref[0:FFN_HALO, :] = hp.astype(BF16)
        h_ref[FFN_HALO:, :] = normed(y_ref[...]).astype(BF16)
        o_ref[...] = y_ref[...]

    u_ref[...] = jnp.dot(h_ref[...], wu_ref[...], preferred_element_type=F32)
    cw = cw_ref[0]
    uc = (cb_ref[0]
          + cw[0:1, :] * u_ref[pl.ds(FFN_HALO - 2, tm), :]
          + cw[1:2, :] * u_ref[pl.ds(FFN_HALO - 1, tm), :]
          + cw[2:3, :] * u_ref[pl.ds(FFN_HALO, tm), :])
    gate = uc[:, :FF_TILE]
    val = uc[:, FF_TILE:]
    act = (gate * jax.nn.sigmoid(gate) * val).astype(BF16)
    o_ref[...] += jnp.dot(act, wd_ref[...], preferred_element_type=F32)


def conv_ffn(y, gain, wu, cw, cb, wd, seq, tm=1024):
    m, d = y.shape
    tm = min(tm, seq)
    nf = D_FF_PAD // FF_TILE
    nprev = tm // FFN_HALO
    return pl.pallas_call(
        functools.partial(_ffn_kernel, tm=tm, seq=seq),
        grid=(m // tm, nf),
        in_specs=[pl.BlockSpec((tm, d), lambda i, j: (i, 0)),
                  pl.BlockSpec((FFN_HALO, d), lambda i, j: (jnp.maximum(i * nprev - 1, 0), 0)),
                  pl.BlockSpec((1, d), lambda i, j: (0, 0)),
                  pl.BlockSpec((d, 2 * FF_TILE), lambda i, j: (0, j)),
                  pl.BlockSpec((1, CONV_W, 2 * FF_TILE), lambda i, j: (j, 0, 0)),
                  pl.BlockSpec((1, 1, 2 * FF_TILE), lambda i, j: (j, 0, 0)),
                  pl.BlockSpec((FF_TILE, d), lambda i, j: (j, 0))],
        out_specs=pl.BlockSpec((tm, d), lambda i, j: (i, 0)),
        out_shape=jax.ShapeDtypeStruct((m, d), F32),
        scratch_shapes=[pltpu.VMEM((tm + FFN_HALO, d), BF16), pltpu.VMEM((tm + FFN_HALO, 2 * FF_TILE), F32)],
        compiler_params=_cparams("parallel", "arbitrary"),
        name="conv_ffn",
    )(y, y, gain.reshape(1, d), wu, cw, cb, wd)


def _pad_cols(w, n):
    return jnp.pad(w, ((0, 0), (0, n - w.shape[1])))


def _prep_w_in(w):
    o = 0
    q, k, v = w[:, o:o + MIX], w[:, o + MIX:o + 2 * MIX], w[:, o + 2 * MIX:o + 3 * MIX]
    o += 3 * MIX
    qi = w[:, o:o + 256]
    ki = _pad_cols(w[:, o + 256:o + 320], LANES)
    wi = _pad_cols(w[:, o + 320:o + 324], LANES)
    seg_a = jnp.concatenate([q, k, v, qi, ki, wi], axis=1)
    o = A_COLS
    rkv = w[:, o:o + 3 * MIX]
    lw = _pad_cols(w[:, o + 3 * MIX:o + 3 * MIX + DECAY_LORA], LANES)
    la = _pad_cols(w[:, o + 3 * MIX + DECAY_LORA:o + 3 * MIX + DECAY_LORA + AAA_LORA], LANES)
    lg = w[:, o + 3 * MIX + DECAY_LORA + AAA_LORA:o + B_COLS]
    seg_b = jnp.concatenate([rkv, lw, la, lg], axis=1)
    o = A_COLS + B_COLS
    seg_cd = w[:, o:o + 2 * MIX]
    seg_g = w[:, o + 2 * MIX:]
    return seg_a.astype(BF16), seg_b.astype(BF16), seg_cd.astype(BF16), seg_g.astype(BF16)


def _prep_mu(mu):
    rkv = mu[:3 * MIX]
    lw = jnp.pad(mu[3 * MIX:3 * MIX + DECAY_LORA], (0, LANES - DECAY_LORA))
    la = jnp.pad(mu[3 * MIX + DECAY_LORA:3 * MIX + DECAY_LORA + AAA_LORA], (0, LANES - AAA_LORA))
    lg = mu[3 * MIX + DECAY_LORA + AAA_LORA:]
    return jnp.concatenate([rkv, lw, la, lg]).reshape(1, 2048)


def _prep_ffn(w_up, conv_w, conv_b, w_down):
    nf = D_FF_PAD // FF_TILE
    pad = D_FF_PAD - D_FF

    def tiles(x):
        gate = jnp.pad(x[..., :D_FF], [(0, 0)] * (x.ndim - 1) + [(0, pad)])
        val = jnp.pad(x[..., D_FF:], [(0, 0)] * (x.ndim - 1) + [(0, pad)])
        lead = x.shape[:-1]
        gate = gate.reshape(*lead, nf, FF_TILE)
        val = val.reshape(*lead, nf, FF_TILE)
        return jnp.concatenate([gate, val], axis=-1)

    wu = tiles(w_up).reshape(D_MODEL, nf * 2 * FF_TILE).astype(BF16)
    cw = jnp.transpose(tiles(conv_w), (1, 0, 2))
    cb = tiles(conv_b).reshape(nf, 1, 2 * FF_TILE)
    wd = jnp.pad(w_down, ((0, pad), (0, 0))).astype(BF16)
    return wu, cw, cb, wd


def kernel(x, positions, norm_mix, w_in, b_gate, idx_k_norm, rwkv_mu, rwkv_w0, rwkv_w2, rwkv_a0, rwkv_a2,
           rwkv_g2, rwkv_k_k, rwkv_k_a, rwkv_r_k, rwkv_lnx_g, rwkv_lnx_b, pool_w, pool_scale, ssm_a_re,
           ssm_a_im, ssm_log_dt, ssm_b_re, ssm_b_im, ssm_c_re, ssm_c_im, ssm_d, ssm_w_glu, ssm_b_glu,
           w_branch, w_out, norm_ffn, w_up, conv_w, conv_b, w_down, norm_final):
    bsz, seq, d = x.shape
    m = bsz * seq
    depth = w_in.shape[0]
    nchunk = seq // S5_CHUNK
    y = x.reshape(m, d)
    cos_t, sin_a, sin_b = rope_tables(positions.astype(F32).reshape(m, 1))

    for l in range(depth):
        seg_a, seg_b, seg_cd, seg_g = _prep_w_in(w_in[l])
        za = norm_matmul(y, norm_mix[l], seg_a)
        zb = norm_matmul(y, norm_mix[l], seg_b)
        zcd = norm_matmul(y, norm_mix[l], seg_cd)
        gates = norm_matmul(y, norm_mix[l], seg_g, bias=b_gate[l], sigmoid=True)

        q, k, v, qi, ki, wi = dsa_prep(za, cos_t, sin_a, sin_b, idx_k_norm[l])
        ya = dsa_attention(q, k, v, qi, ki, wi, bsz, seq)

        pad_rows = lambda w: jnp.pad(w, ((0, LANES - w.shape[0]), (0, 0)))
        rp = dict(mu=_prep_mu(rwkv_mu[l]), w0=rwkv_w0[l].reshape(1, MIX), w2=pad_rows(rwkv_w2[l]),
                  a0=rwkv_a0[l].reshape(1, MIX), a2=pad_rows(rwkv_a2[l]), g2=rwkv_g2[l],
                  k_k=rwkv_k_k[l].reshape(1, MIX), k_a=rwkv_k_a[l].reshape(1, MIX),
                  r_k=rwkv_r_k[l].reshape(1, MIX))
        r_, w_, k_, v_, a_, b_, g_, bonus = rwkv_prep(zb, rp, seq)
        coef = [_to_scan_coef(t, bsz, seq) for t in (r_, w_, k_, a_, b_)]
        ysc = rwkv_scan(*coef, _to_scan_val(v_, bsz, seq))
        yb = rwkv_post(_from_scan_val(ysc, bsz, seq), bonus, g_, rwkv_lnx_g[l], rwkv_lnx_b[l])

        yc = multiscale_pool(zcd, pool_w[l], pool_scale[l], bsz, seq)

        m1, s_in, s_out, l1, l2 = _s5_matrices(ssm_a_re[l], ssm_a_im[l], ssm_log_dt[l], ssm_b_re[l],
                                               ssm_b_im[l], ssm_c_re[l], ssm_c_im[l])
        uf = zcd[:, MIX:].reshape(bsz, nchunk, S5_CHUNK, SSM_GROUPS, SSM_GROUP_DIM)
        uf = jnp.transpose(uf, (3, 0, 1, 2, 4)).reshape(SSM_GROUPS, bsz * nchunk, S5_CHUNK * SSM_GROUP_DIM)
        ycore = s5_core(uf.astype(BF16), m1.astype(BF16), s_in.astype(BF16), s_out.astype(BF16), l1, l2, nchunk)
        ycore = ycore.reshape(SSM_GROUPS, bsz, nchunk, S5_CHUNK, SSM_GROUP_DIM)
        ycore = jnp.transpose(ycore, (1, 2, 3, 0, 4)).reshape(m, MIX)
        yd = s5_post(ycore, zcd, ssm_d[l], ssm_w_glu[l], ssm_b_glu[l])

        merged = gated_merge(ya, yb, yc, yd, gates, w_branch[l].astype(BF16))
        y = matmul_residual(merged, w_out[l].astype(BF16), y)

        wu, cw, cb, wd = _prep_ffn(w_up[l], conv_w[l], conv_b[l], w_down[l])
        y = conv_ffn(y, norm_ffn[l], wu, cw, cb, wd, seq)

    return rms_norm_final(y, norm_final).reshape(bsz, seq, d)
```
